```python
import jax, jax.numpy as jnp
from jax import lax
import numpy as np

D_MODEL = 4096
BATCH = 4
SEQ = 4096
DEPTH = 2

CHUNK = 64
Q_BLOCK = 128
EPS = 1e-6

SB_HEADS = 16
SB_HEAD_DIM = D_MODEL // 32
SB_WIDTH = SB_HEADS * SB_HEAD_DIM
POOL_GROUPS = 4
POOL_WIDTH = D_MODEL // 2
POOL_GROUP_DIM = POOL_WIDTH // POOL_GROUPS
POOL_WINDOWS = (2, 4, 8, 16)
EVEN_GATE = SB_WIDTH + POOL_WIDTH
EVEN_SIZES = (SB_WIDTH, SB_WIDTH, SB_WIDTH, POOL_WIDTH, EVEN_GATE)
EVEN_IN = sum(EVEN_SIZES)

FOX_HEADS = 16
FOX_HEAD_DIM = D_MODEL // 32
FOX_WIDTH = FOX_HEADS * FOX_HEAD_DIM
RET_HEADS = 8
RET_HEAD_DIM = D_MODEL // 16
RET_WIDTH = RET_HEADS * RET_HEAD_DIM
ROPE_BASE = 10000.0
ODD_GATE = FOX_WIDTH + RET_WIDTH
ODD_SIZES = (FOX_WIDTH, FOX_WIDTH, FOX_WIDTH, FOX_HEADS, RET_WIDTH, RET_WIDTH, RET_WIDTH, ODD_GATE)
ODD_IN = sum(ODD_SIZES)

kernel_name = 'hybrid_stickbreak_pool_fox_retention_trunk'


def _split(t, sizes):
    idx = [int(i) for i in np.cumsum(sizes)[:-1]]
    return jnp.split(t, idx, axis=-1)


def rmsnorm(x, g):
    x32 = x.astype(jnp.float32)
    y = x32 * lax.rsqrt(jnp.mean(x32 * x32, axis=-1, keepdims=True) + EPS)
    return y.astype(x.dtype) * g


def to_heads(t, n):
    b, s, _ = t.shape
    return t.reshape(b, s, n, -1).transpose(0, 2, 1, 3)


def from_heads(t):
    b, h, s, d = t.shape
    return t.transpose(0, 2, 1, 3).reshape(b, s, h * d)


def stick_breaking_attention(q, k, v):
    s_len, d = q.shape[2], q.shape[3]
    scale = d ** -0.5
    outs = []
    for start in range(0, s_len, Q_BLOCK):
        end = start + Q_BLOCK
        z = jnp.einsum('bhqd,bhkd->bhqk', q[:, :, start:end], k[:, :, :end]).astype(jnp.float32) * scale
        t_pos = start + jnp.arange(Q_BLOCK)[:, None]
        s_pos = jnp.arange(end)[None, :]
        valid = s_pos < t_pos
        log_beta = jax.nn.log_sigmoid(z)
        log_rest = jnp.where(valid, jax.nn.log_sigmoid(-z), 0.0)
        after = lax.cumsum(log_rest, axis=3, reverse=True) - log_rest
        w = jnp.where(valid, jnp.exp(log_beta + after), 0.0)
        outs.append(jnp.einsum('bhqk,bhkd->bhqd', w.astype(v.dtype), v[:, :, :end]))
    return jnp.concatenate(outs, axis=2)


def multiscale_pool(u, pool_w, pool_scale):
    b, s_len, _ = u.shape
    u32 = u.astype(jnp.float32)
    cs = jnp.concatenate([jnp.zeros((b, 1, POOL_WIDTH), jnp.float32), jnp.cumsum(u32, axis=1)], axis=1)
    csg = cs.reshape(b, s_len + 1, POOL_GROUPS, POOL_GROUP_DIM)
    ug = u32.reshape(b, s_len, POOL_GROUPS, POOL_GROUP_DIM)
    t = jnp.arange(s_len)
    pooled = []
    for g, win in enumerate(POOL_WINDOWS):
        lo = jnp.maximum(t + 1 - win, 0)
        win_sum = csg[:, 1:, g] - csg[:, lo, g]
        cnt = (t + 1 - lo).astype(jnp.float32)[None, :, None]
        pooled.append(win_sum / cnt - ug[:, :, g])
    p = jnp.stack(pooled, axis=2).astype(u.dtype)
    y = jnp.einsum('bsgc,gcd->bsgd', p, pool_w)
    return y.reshape(b, s_len, POOL_WIDTH) * pool_scale


def head_rms(t, g):
    t32 = t.astype(jnp.float32)
    y = t32 * lax.rsqrt(jnp.mean(t32 * t32, axis=-1, keepdims=True) + EPS)
    return y.astype(t.dtype) * g


def forgetting_attention(q, k, v, log_f):
    s_len, d = q.shape[2], q.shape[3]
    scale = d ** -0.5
    cum_f = jnp.cumsum(log_f, axis=-1)
    outs = []
    for start in range(0, s_len, Q_BLOCK):
        end = start + Q_BLOCK
        logits = jnp.einsum('bhqd,bhkd->bhqk', q[:, :, start:end], k[:, :, :end]).astype(jnp.float32) * scale
        logits = logits + cum_f[:, :, start:end, None] - cum_f[:, :, None, :end]
        valid = jnp.arange(end)[None, :] <= (start + jnp.arange(Q_BLOCK))[:, None]
        p = jax.nn.softmax(jnp.where(valid, logits, -jnp.inf), axis=-1)
        outs.append(jnp.einsum('bhqk,bhkd->bhqd', p.astype(v.dtype), v[:, :, :end]))
    return jnp.concatenate(outs, axis=2)


def rotary(t, pos):
    d = t.shape[-1]
    inv = ROPE_BASE ** (-jnp.arange(0, d, 2, dtype=jnp.float32) / d)
    ang = pos.astype(jnp.float32)[:, None] * inv[None, :]
    cos, sin = jnp.cos(ang), jnp.sin(ang)
    t1, t2 = t[..., : d // 2], t[..., d // 2 :]
    return jnp.concatenate([t1 * cos - t2 * sin, t2 * cos + t1 * sin], axis=-1)


def retention(q, k, v):
    b, h, s_len, d = q.shape
    nc = s_len // CHUNK
    q = q.astype(jnp.float32)
    k = k.astype(jnp.float32) * d ** -0.5
    v = v.astype(jnp.float32)
    log_g = jnp.log(1.0 - 2.0 ** (-5.0 - jnp.arange(h, dtype=jnp.float32)))
    idx = jnp.arange(CHUNK, dtype=jnp.float32)
    intra_decay = jnp.exp(log_g[:, None, None] * jnp.abs(idx[:, None] - idx[None, :]))
    q_decay = jnp.exp(log_g[:, None] * (idx + 1.0))[..., None]
    k_decay = jnp.exp(log_g[:, None] * (CHUNK - 1.0 - idx))[..., None]
    chunk_decay = jnp.exp(log_g * CHUNK)[:, None, None]

    def chunks(t):
        return t.reshape(b, h, nc, CHUNK, d).transpose(2, 0, 1, 3, 4)

    def step(state, qkv):
        qc, kc, vc = qkv
        scores = jnp.einsum('bhid,bhjd->bhij', qc, kc) * intra_decay
        out = jnp.einsum('bhij,bhjd->bhid', scores, vc) + jnp.einsum('bhid,bhde->bhie', qc * q_decay, state)
        state = state * chunk_decay + jnp.einsum('bhjd,bhje->bhde', kc * k_decay, vc)
        return state, out

    state0 = jnp.zeros((b, h, d, d), jnp.float32)
    _, outs = lax.scan(step, state0, (chunks(q), chunks(k), chunks(v)))
    return outs.transpose(1, 2, 0, 3, 4).reshape(b, h, s_len, d)


def even_layer(x, norm_g, w_in, pool_w, pool_scale, w_out):
    h = rmsnorm(x, norm_g)
    q, k, v, u, gate = _split(h @ w_in, EVEN_SIZES)
    o_a = from_heads(stick_breaking_attention(to_heads(q, SB_HEADS), to_heads(k, SB_HEADS), to_heads(v, SB_HEADS)))
    o_b = multiscale_pool(u, pool_w, pool_scale)
    y = jnp.concatenate([o_a, o_b], axis=-1) * jax.nn.silu(gate)
    return x + y @ w_out


def odd_layer(x, norm_g, w_in, b_f, q_norm_g, k_norm_g, ret_norm_g, w_out):
    s_len = x.shape[1]
    h = rmsnorm(x, norm_g)
    qc, kc, vc, f_logit, qd, kd, vd, gate = _split(h @ w_in, ODD_SIZES)
    log_f = jax.nn.log_sigmoid((f_logit + b_f).astype(jnp.float32)).transpose(0, 2, 1)
    qc = head_rms(to_heads(qc, FOX_HEADS), q_norm_g)
    kc = head_rms(to_heads(kc, FOX_HEADS), k_norm_g)
    o_c = from_heads(forgetting_attention(qc, kc, to_heads(vc, FOX_HEADS), log_f))
    pos = jnp.arange(s_len)
    qd = rotary(to_heads(qd, RET_HEADS).astype(jnp.float32), pos)
    kd = rotary(to_heads(kd, RET_HEADS).astype(jnp.float32), pos)
    o_d = retention(qd, kd, to_heads(vd, RET_HEADS))
    o_d = from_heads(head_rms(o_d, 1.0)).astype(x.dtype) * ret_norm_g
    y = jnp.concatenate([o_c, o_d], axis=-1) * jax.nn.silu(gate)
    return x + y @ w_out


def setup_inputs(seed: int = 0) -> dict:
    key = jax.random.key(seed)
    ks = jax.random.split(key, 14)
    f32 = jnp.float32

    def nrm(k, shape, scale):
        return jax.random.normal(k, shape, f32) * scale

    return {
        'x': nrm(ks[0], (BATCH, SEQ, D_MODEL), 1.0),
        'l0_norm_g': 1.0 + nrm(ks[1], (D_MODEL,), 0.02),
        'l0_w_in': nrm(ks[2], (D_MODEL, EVEN_IN), D_MODEL ** -0.5),
        'l0_pool_w': nrm(ks[3], (POOL_GROUPS, POOL_GROUP_DIM, POOL_GROUP_DIM), POOL_GROUP_DIM ** -0.5),
        'l0_pool_scale': 1.0 + nrm(ks[4], (POOL_WIDTH,), 0.1),
        'l0_w_out': nrm(ks[5], (EVEN_GATE, D_MODEL), EVEN_GATE ** -0.5),
        'l1_norm_g': 1.0 + nrm(ks[6], (D_MODEL,), 0.02),
        'l1_w_in': nrm(ks[7], (D_MODEL, ODD_IN), D_MODEL ** -0.5),
        'l1_b_f': 3.0 + nrm(ks[8], (FOX_HEADS,), 0.5),
        'l1_q_norm_g': 1.0 + nrm(ks[9], (FOX_HEAD_DIM,), 0.02),
        'l1_k_norm_g': 1.0 + nrm(ks[10], (FOX_HEAD_DIM,), 0.02),
        'l1_ret_norm_g': 1.0 + nrm(ks[11], (RET_WIDTH,), 0.02),
        'l1_w_out': nrm(ks[12], (ODD_GATE, D_MODEL), ODD_GATE ** -0.5),
    }


def reference(x, l0_norm_g, l0_w_in, l0_pool_w, l0_pool_scale, l0_w_out,
              l1_norm_g, l1_w_in, l1_b_f, l1_q_norm_g, l1_k_norm_g, l1_ret_norm_g, l1_w_out):
    params = (
        (l0_norm_g, l0_w_in, l0_pool_w, l0_pool_scale, l0_w_out),
        (l1_norm_g, l1_w_in, l1_b_f, l1_q_norm_g, l1_k_norm_g, l1_ret_norm_g, l1_w_out),
    )
    for layer in range(DEPTH):
        if layer % 2 == 0:
            x = even_layer(x, *params[layer])
        else:
            x = odd_layer(x, *params[layer])
    return x
```

```python
import functools
import math

import numpy as np
import jax
import jax.numpy as jnp
from jax import lax
from jax.experimental import pallas as pl
from jax.experimental.pallas import tpu as pltpu

F32 = jnp.float32
BF16 = jnp.bfloat16

EPS = 1e-6
ROPE_BASE = 10000.0
LANES = 128
HEAD_DIM = 128
RET_HEAD_DIM = 256
RET_BLOCK = 256
RET_CHUNK = 64
POOL_WINDOWS = (2, 4, 8, 16)
POOL_HALO = 16
ATT_BLOCK = 128
VMEM_LIMIT = 48 * 1024 * 1024
NEG_BIG = -1e30


def _cparams(sem):
    return pltpu.CompilerParams(dimension_semantics=sem, vmem_limit_bytes=VMEM_LIMIT)


def _rmsnorm_kernel(x_ref, g_ref, o_ref):
    x = x_ref[...]
    ms = jnp.mean(x * x, axis=-1, keepdims=True)
    o_ref[...] = ((x * lax.rsqrt(ms + EPS)) * g_ref[...]).astype(o_ref.dtype)


def _rmsnorm(x, g, tm=256):
    t, d = x.shape
    return pl.pallas_call(
        _rmsnorm_kernel,
        out_shape=jax.ShapeDtypeStruct((t, d), BF16),
        grid=(t // tm,),
        in_specs=[pl.BlockSpec((tm, d), lambda i: (i, 0)),
                  pl.BlockSpec((1, d), lambda i: (0, 0))],
        out_specs=pl.BlockSpec((tm, d), lambda i: (i, 0)),
        compiler_params=_cparams(("arbitrary",)),
        name="rmsnorm",
    )(x, g.reshape(1, d))


def _proj_kernel(a_ref, b_ref, o_ref):
    o_ref[...] = jnp.dot(a_ref[...], b_ref[...], preferred_element_type=F32).astype(o_ref.dtype)


def _proj(a, b, out_dtype, tm=512, tn=1024):
    t, k = a.shape
    n = b.shape[1]
    tn = min(tn, n)
    return pl.pallas_call(
        _proj_kernel,
        out_shape=jax.ShapeDtypeStruct((t, n), out_dtype),
        grid=(n // tn, t // tm),
        in_specs=[pl.BlockSpec((tm, k), lambda j, i: (i, 0)),
                  pl.BlockSpec((k, tn), lambda j, i: (0, j))],
        out_specs=pl.BlockSpec((tm, tn), lambda j, i: (i, j)),
        compiler_params=_cparams(("arbitrary", "arbitrary")),
        name="proj",
    )(a, b)


def _outproj_kernel(ya_ref, yb_ref, wa_ref, wb_ref, x_ref, o_ref):
    acc = jnp.dot(ya_ref[...], wa_ref[...], preferred_element_type=F32)
    acc = acc + jnp.dot(yb_ref[...], wb_ref[...], preferred_element_type=F32)
    o_ref[...] = x_ref[...] + acc


def _outproj(ya, yb, w, x, tm=512, tn=1024):
    t, kh = ya.shape
    n = w.shape[1]
    return pl.pallas_call(
        _outproj_kernel,
        out_shape=jax.ShapeDtypeStruct((t, n), F32),
        grid=(n // tn, t // tm),
        in_specs=[pl.BlockSpec((tm, kh), lambda j, i: (i, 0)),
                  pl.BlockSpec((tm, kh), lambda j, i: (i, 0)),
                  pl.BlockSpec((kh, tn), lambda j, i: (0, j)),
                  pl.BlockSpec((kh, tn), lambda j, i: (1, j)),
                  pl.BlockSpec((tm, tn), lambda j, i: (i, j))],
        out_specs=pl.BlockSpec((tm, tn), lambda j, i: (i, j)),
        compiler_params=_cparams(("arbitrary", "arbitrary")),
        name="outproj",
    )(ya, yb, w, w, x)


def _silu(g):
    return g / (1.0 + jnp.exp(-g))


def _dot_nt(a, b):
    return lax.dot_general(a, b, (((1,), (1,)), ((), ())), preferred_element_type=F32)


def _sb_kernel(q_ref, k_ref, v_ref, g_ref, o_ref, *, scale):
    blk = ATT_BLOCK
    i = pl.program_id(2)
    q = q_ref[...]

    r_i = lax.broadcasted_iota(jnp.int32, (2 * blk, 2 * blk), 0) % blk
    c_i = lax.broadcasted_iota(jnp.int32, (2 * blk, 2 * blk), 1)
    suffix = jnp.where((c_i >= blk) | (r_i > c_i), 1.0, 0.0).astype(BF16)

    row = lax.broadcasted_iota(jnp.int32, (blk, blk), 0)
    col = lax.broadcasted_iota(jnp.int32, (blk, blk), 1)
    causal = col < row

    def tile(kb, vb, run, acc, mask):
        z = _dot_nt(q, kb) * scale
        log_beta = jnp.minimum(z, 0.0) - jnp.log1p(jnp.exp(-jnp.abs(z)))
        log_rest = log_beta - z
        if mask is not None:
            log_rest = jnp.where(mask, log_rest, 0.0)
        hi = log_rest.astype(BF16)
        mid = (log_rest - hi.astype(F32)).astype(BF16)
        c = jnp.dot(jnp.concatenate([hi, mid], axis=1), suffix, preferred_element_type=F32)
        w = jnp.exp(log_beta + c[:, :blk] + run)
        if mask is not None:
            w = jnp.where(mask, w, 0.0)
        acc = acc + jnp.dot(w.astype(BF16), vb, preferred_element_type=F32)
        return run + c[:, blk:], acc

    d0 = pl.multiple_of(i * blk, blk)
    run, acc = tile(k_ref[pl.ds(d0, blk), :], v_ref[pl.ds(d0, blk), :],
                    jnp.zeros((blk, blk), F32), jnp.zeros((blk, HEAD_DIM), F32), causal)

    def body(n, carry):
        run, acc = carry
        s0 = pl.multiple_of((i - 1 - n) * blk, blk)
        return tile(k_ref[pl.ds(s0, blk), :], v_ref[pl.ds(s0, blk), :], run, acc, None)

    run, acc = lax.fori_loop(0, i, body, (run, acc))
    o_ref[...] = (acc * _silu(g_ref[...].astype(F32))).astype(o_ref.dtype)


def _sb_attention(p, batch, seq, heads, q_col, k_col, v_col, g_col):
    blk = ATT_BLOCK
    nq = seq // blk
    kern = functools.partial(_sb_kernel, scale=HEAD_DIM ** -0.5)
    return pl.pallas_call(
        kern,
        out_shape=jax.ShapeDtypeStruct((batch * seq, heads * HEAD_DIM), BF16),
        grid=(batch, heads, nq),
        in_specs=[pl.BlockSpec((blk, HEAD_DIM), lambda b, h, i: (b * nq + i, q_col + h)),
                  pl.BlockSpec((seq, HEAD_DIM), lambda b, h, i: (b, k_col + h)),
                  pl.BlockSpec((seq, HEAD_DIM), lambda b, h, i: (b, v_col + h)),
                  pl.BlockSpec((blk, HEAD_DIM), lambda b, h, i: (b * nq + i, g_col + h))],
        out_specs=pl.BlockSpec((blk, HEAD_DIM), lambda b, h, i: (b * nq + i, h)),
        compiler_params=_cparams(("arbitrary", "arbitrary", "arbitrary")),
        name="stickbreak_attn",
    )(p, p, p, p)


def _pool_kernel(u_ref, halo_ref, g_ref, w_ref, s_ref, o_ref, ext_ref, *, ts, gdim):
    t = pl.program_id(1)
    halo = halo_ref[...].astype(F32)
    ext_ref[0:POOL_HALO, :] = jnp.where(t > 0, halo, 0.0)
    ext_ref[POOL_HALO:POOL_HALO + ts, :] = u_ref[...].astype(F32)
    pos = t * ts + lax.broadcasted_iota(jnp.int32, (ts, 1), 0)
    for g, win in enumerate(POOL_WINDOWS):
        cs = slice(g * gdim, (g + 1) * gdim)
        cur = ext_ref[POOL_HALO:POOL_HALO + ts, cs]
        tot = cur
        for k in range(1, win):
            tot = tot + ext_ref[POOL_HALO - k:POOL_HALO - k + ts, cs]
        cnt = jnp.minimum(pos + 1, win).astype(F32)
        pooled = tot / cnt - cur
        y = jnp.dot(pooled.astype(BF16), w_ref[g], preferred_element_type=F32)
        y = y * s_ref[:, cs] * _silu(g_ref[:, cs].astype(F32))
        o_ref[:, cs] = y.astype(o_ref.dtype)


def _pool(p, pool_w, pool_scale, batch, seq, u_col, g_col, ts=512):
    groups, gdim, _ = pool_w.shape
    width = groups * gdim
    nts = seq // ts
    halo_blocks = ts // POOL_HALO
    kern = functools.partial(_pool_kernel, ts=ts, gdim=gdim)
    return pl.pallas_call(
        kern,
        out_shape=jax.ShapeDtypeStruct((batch * seq, width), BF16),
        grid=(batch, nts),
        in_specs=[pl.BlockSpec((ts, width), lambda b, t: (b * nts + t, u_col)),
                  pl.BlockSpec((POOL_HALO, width),
                               lambda b, t: (jnp.maximum((b * nts + t) * halo_blocks - 1, 0), u_col)),
                  pl.BlockSpec((ts, width), lambda b, t: (b * nts + t, g_col)),
                  pl.BlockSpec((groups, gdim, gdim), lambda b, t: (0, 0, 0)),
                  pl.BlockSpec((1, width), lambda b, t: (0, 0))],
        out_specs=pl.BlockSpec((ts, width), lambda b, t: (b * nts + t, 0)),
        scratch_shapes=[pltpu.VMEM((ts + POOL_HALO, width), F32)],
        compiler_params=_cparams(("arbitrary", "arbitrary")),
        name="multiscale_pool",
    )(p, p, p, pool_w, pool_scale.reshape(1, width))


def _headnorm_kernel(q_ref, k_ref, gq_ref, gk_ref, qo_ref, ko_ref, *, heads):
    for src, g_ref, dst in ((q_ref, gq_ref, qo_ref), (k_ref, gk_ref, ko_ref)):
        g = g_ref[...]
        for h in range(heads):
            cs = slice(h * HEAD_DIM, (h + 1) * HEAD_DIM)
            x = src[:, cs].astype(F32)
            ms = jnp.mean(x * x, axis=-1, keepdims=True)
            dst[:, cs] = ((x * lax.rsqrt(ms + EPS)) * g).astype(dst.dtype)


def _headnorm(p, gq, gk, heads, q_col, k_col, tm=512):
    t = p.shape[0]
    width = heads * HEAD_DIM
    kern = functools.partial(_headnorm_kernel, heads=heads)
    return pl.pallas_call(
        kern,
        out_shape=(jax.ShapeDtypeStruct((t, width), BF16), jax.ShapeDtypeStruct((t, width), BF16)),
        grid=(t // tm,),
        in_specs=[pl.BlockSpec((tm, width), lambda i: (i, q_col)),
                  pl.BlockSpec((tm, width), lambda i: (i, k_col)),
                  pl.BlockSpec((1, HEAD_DIM), lambda i: (0, 0)),
                  pl.BlockSpec((1, HEAD_DIM), lambda i: (0, 0))],
        out_specs=(pl.BlockSpec((tm, width), lambda i: (i, 0)),
                   pl.BlockSpec((tm, width), lambda i: (i, 0))),
        compiler_params=_cparams(("arbitrary",)),
        name="head_rmsnorm",
    )(p, p, gq.reshape(1, HEAD_DIM), gk.reshape(1, HEAD_DIM))


def _split3(x):
    hi = x.astype(BF16)
    r1 = x - hi.astype(F32)
    mid = r1.astype(BF16)
    lo = (r1 - mid.astype(F32)).astype(BF16)
    return hi, mid, lo


def _fcum_kernel(f_ref, b_ref, col_ref, row_ref, *, seq):
    blk = LANES
    r = lax.broadcasted_iota(jnp.int32, (blk, 3 * blk), 0)
    c = lax.broadcasted_iota(jnp.int32, (blk, 3 * blk), 1) % blk
    lower = jnp.where(c <= r, 1.0, 0.0).astype(BF16)
    carry = jnp.zeros((1, blk), F32)
    for n in range(seq // blk):
        rows = slice(n * blk, (n + 1) * blk)
        x = f_ref[rows, :] + b_ref[...]
        log_f = jnp.minimum(x, 0.0) - jnp.log1p(jnp.exp(-jnp.abs(x)))
        parts = jnp.concatenate(_split3(log_f), axis=0)
        cum = jnp.dot(lower, parts, preferred_element_type=F32) + carry
        col_ref[rows, :] = cum
        row_ref[:, rows] = cum.T
        carry = cum[blk - 1:blk, :]


def _fcum(f, b_pad, batch, seq):
    kern = functools.partial(_fcum_kernel, seq=seq)
    return pl.pallas_call(
        kern,
        out_shape=(jax.ShapeDtypeStruct((batch * seq, LANES), F32),
                   jax.ShapeDtypeStruct((batch * LANES, seq), F32)),
        grid=(batch,),
        in_specs=[pl.BlockSpec((seq, LANES), lambda b: (b, 0)),
                  pl.BlockSpec((1, LANES), lambda b: (0, 0))],
        out_specs=(pl.BlockSpec((seq, LANES), lambda b: (b, 0)),
                   pl.BlockSpec((LANES, seq), lambda b: (b, 0))),
        compiler_params=_cparams(("arbitrary",)),
        name="forget_cumsum",
    )(f, b_pad)


def _fox_kernel(q_ref, k_ref, v_ref, fcol_ref, frow_ref, g_ref, o_ref, *, scale):
    blk = ATT_BLOCK
    h = pl.program_id(1)
    i = pl.program_id(2)
    q = q_ref[...]
    lane = lax.broadcasted_iota(jnp.int32, (blk, LANES), 1)
    f_t = jnp.sum(jnp.where(lane == h, fcol_ref[...], 0.0), axis=1, keepdims=True)

    row = lax.broadcasted_iota(jnp.int32, (blk, blk), 0)
    col = lax.broadcasted_iota(jnp.int32, (blk, blk), 1)
    causal = col <= row

    def tile(j, s0, m, l, acc, mask):
        kb = k_ref[pl.ds(s0, blk), :]
        vb = v_ref[pl.ds(s0, blk), :]
        f_s = frow_ref[0, pl.ds(j, 1), :]
        z = _dot_nt(q, kb) * scale + (f_t - f_s)
        if mask is not None:
            z = jnp.where(mask, z, NEG_BIG)
        m_new = jnp.maximum(m, jnp.max(z, axis=1, keepdims=True))
        alpha = jnp.exp(m - m_new)
        p = jnp.exp(z - m_new)
        l = l * alpha + jnp.sum(p, axis=1, keepdims=True)
        acc = acc * alpha + jnp.dot(p.astype(BF16), vb, preferred_element_type=F32)
        return m_new, l, acc

    m, l, acc = tile(i, pl.multiple_of(i * blk, blk),
                     jnp.full((blk, 1), NEG_BIG, F32), jnp.zeros((blk, 1), F32),
                     jnp.zeros((blk, HEAD_DIM), F32), causal)

    def body(j, carry):
        return tile(j, pl.multiple_of(j * blk, blk), *carry, None)

    m, l, acc = lax.fori_loop(0, i, body, (m, l, acc))
    o_ref[...] = ((acc / l) * _silu(g_ref[...].astype(F32))).astype(o_ref.dtype)


def _fox_attention(qn, kn, p, fcol, frow, batch, seq, heads, v_col, g_col):
    blk = ATT_BLOCK
    nq = seq // blk
    kern = functools.partial(_fox_kernel, scale=HEAD_DIM ** -0.5)
    frow3 = frow.reshape(batch * LANES, nq, blk)
    return pl.pallas_call(
        kern,
        out_shape=jax.ShapeDtypeStruct((batch * seq, heads * HEAD_DIM), BF16),
        grid=(batch, heads, nq),
        in_specs=[pl.BlockSpec((blk, HEAD_DIM), lambda b, h, i: (b * nq + i, h)),
                  pl.BlockSpec((seq, HEAD_DIM), lambda b, h, i: (b, h)),
                  pl.BlockSpec((seq, HEAD_DIM), lambda b, h, i: (b, v_col + h)),
                  pl.BlockSpec((blk, LANES), lambda b, h, i: (b * nq + i, 0)),
                  pl.BlockSpec((1, nq, blk), lambda b, h, i: (b * LANES + h, 0, 0)),
                  pl.BlockSpec((blk, HEAD_DIM), lambda b, h, i: (b * nq + i, g_col + h))],
        out_specs=pl.BlockSpec((blk, HEAD_DIM), lambda b, h, i: (b * nq + i, h)),
        compiler_params=_cparams(("arbitrary", "arbitrary", "arbitrary")),
        name="forgetting_attn",
    )(qn, kn, p, fcol, frow3, p)


def _ret_kernel(q_ref, k_ref, v_ref, cos_ref, sin_ref, g_ref, ng_ref, o_ref, state_ref, decay_ref):
    L = RET_BLOCK
    half = RET_HEAD_DIM // 2
    h = pl.program_id(1)
    t = pl.program_id(2)
    hf = jnp.full((1, 1), h, jnp.int32).astype(F32)
    log_g = jnp.log(1.0 - jnp.exp2(-5.0 - hf))

    @pl.when(t == 0)
    def _():
        state_ref[...] = jnp.zeros_like(state_ref)
        n = lax.broadcasted_iota(jnp.int32, (L, L), 0)
        m = lax.broadcasted_iota(jnp.int32, (L, L), 1)
        dist = jnp.abs(n - m).astype(F32)
        decay_ref[...] = jnp.where(m // RET_CHUNK <= n // RET_CHUNK, jnp.exp(log_g * dist), 0.0)

    cos = cos_ref[...]
    sin = sin_ref[...]

    def rot(x):
        x1, x2 = x[:, :half], x[:, half:]
        return jnp.concatenate([x1 * cos - x2 * sin, x2 * cos + x1 * sin], axis=1)

    q = rot(q_ref[...].astype(F32))
    k = rot(k_ref[...].astype(F32)) * (RET_HEAD_DIM ** -0.5)
    v = v_ref[...]
    idx = lax.broadcasted_iota(jnp.int32, (L, 1), 0).astype(F32)
    q_dec = jnp.exp(log_g * (idx + 1.0))
    k_dec = jnp.exp(log_g * (L - 1.0 - idx))

    scores = _dot_nt(q.astype(BF16), k.astype(BF16)) * decay_ref[...]
    state = state_ref[...]
    out = jnp.dot(scores.astype(BF16), v, preferred_element_type=F32)
    out = out + jnp.dot((q * q_dec).astype(BF16), state.astype(BF16), preferred_element_type=F32)
    kd_t = (k * k_dec).astype(BF16).T
    state_ref[...] = state * jnp.exp(log_g * L) + jnp.dot(kd_t, v, preferred_element_type=F32)

    ms = jnp.mean(out * out, axis=-1, keepdims=True)
    y = (out * lax.rsqrt(ms + EPS)) * ng_ref[...]
    o_ref[...] = (y * _silu(g_ref[...].astype(F32))).astype(o_ref.dtype)


def _rope_tables(seq, dim):
    inv = ROPE_BASE ** (-np.arange(0, dim, 2, dtype=np.float64) / dim)
    ang = np.arange(seq, dtype=np.float64)[:, None] * inv[None, :]
    return jnp.asarray(np.cos(ang), F32), jnp.asarray(np.sin(ang), F32)


def _retention(p, ng, batch, seq, heads, q_col, k_col, v_col, g_col):
    L = RET_BLOCK
    d = RET_HEAD_DIM
    nt = seq // L
    cos, sin = _rope_tables(seq, d)
    return pl.pallas_call(
        _ret_kernel,
        out_shape=jax.ShapeDtypeStruct((batch * seq, heads * d), BF16),
        grid=(batch, heads, nt),
        in_specs=[pl.BlockSpec((L, d), lambda b, h, t: (b * nt + t, q_col + h)),
                  pl.BlockSpec((L, d), lambda b, h, t: (b * nt + t, k_col + h)),
                  pl.BlockSpec((L, d), lambda b, h, t: (b * nt + t, v_col + h)),
                  pl.BlockSpec((L, d // 2), lambda b, h, t: (t, 0)),
                  pl.BlockSpec((L, d // 2), lambda b, h, t: (t, 0)),
                  pl.BlockSpec((L, d), lambda b, h, t: (b * nt + t, g_col + h)),
                  pl.BlockSpec((1, d), lambda b, h, t: (0, h))],
        out_specs=pl.BlockSpec((L, d), lambda b, h, t: (b * nt + t, h)),
        scratch_shapes=[pltpu.VMEM((d, d), F32), pltpu.VMEM((L, L), F32)],
        compiler_params=_cparams(("arbitrary", "arbitrary", "arbitrary")),
        name="retention",
    )(p, p, p, cos, sin, p, ng.reshape(1, heads * d))


def _even_layer(x2, batch, seq, norm_g, w_in, pool_w, pool_scale, w_out):
    d = x2.shape[1]
    sb_width = d // 2
    heads = sb_width // HEAD_DIM
    pool_width = pool_w.shape[0] * pool_w.shape[1]
    h = _rmsnorm(x2, norm_g)
    p = _proj(h, w_in.astype(BF16), BF16)
    cb = sb_width // HEAD_DIM
    ya = _sb_attention(p, batch, seq, heads, 0, cb, 2 * cb, (3 * sb_width + pool_width) // HEAD_DIM)
    yb = _pool(p, pool_w.astype(BF16), pool_scale, batch, seq,
               (3 * sb_width) // pool_width, (3 * sb_width + pool_width + sb_width) // pool_width)
    return _outproj(ya, yb, w_out.astype(BF16), x2)


def _odd_layer(x2, batch, seq, norm_g, w_in, b_f, q_norm_g, k_norm_g, ret_norm_g, w_out):
    d = x2.shape[1]
    fox_width = d // 2
    fox_heads = fox_width // HEAD_DIM
    ret_width = ret_norm_g.shape[0]
    ret_heads = ret_width // RET_HEAD_DIM
    f0 = 3 * fox_width
    h = _rmsnorm(x2, norm_g)
    w_main = jnp.concatenate([w_in[:, :f0], w_in[:, f0 + fox_heads:]], axis=1).astype(BF16)
    w_f = jnp.pad(w_in[:, f0:f0 + fox_heads], ((0, 0), (0, LANES - fox_heads))).astype(BF16)
    p = _proj(h, w_main, BF16)
    f_logit = _proj(h, w_f, F32)
    fcol, frow = _fcum(f_logit, jnp.pad(b_f, (0, LANES - fox_heads)).reshape(1, LANES), batch, seq)
    cb = fox_width // HEAD_DIM
    qn, kn = _headnorm(p, q_norm_g, k_norm_g, fox_heads, 0, 1)
    gate0 = 3 * fox_width + 3 * ret_width
    yc = _fox_attention(qn, kn, p, fcol, frow, batch, seq, fox_heads, 2 * cb, gate0 // HEAD_DIM)
    rb = f0 // RET_HEAD_DIM
    rh = ret_width // RET_HEAD_DIM
    yd = _retention(p, ret_norm_g, batch, seq, ret_heads, rb, rb + rh, rb + 2 * rh,
                    (gate0 + fox_width) // RET_HEAD_DIM)
    return _outproj(yc, yd, w_out.astype(BF16), x2)


def kernel(x, l0_norm_g, l0_w_in, l0_pool_w, l0_pool_scale, l0_w_out,
           l1_norm_g, l1_w_in, l1_b_f, l1_q_norm_g, l1_k_norm_g, l1_ret_norm_g, l1_w_out):
    batch, seq, d = x.shape
    x2 = x.reshape(batch * seq, d)
    x2 = _even_layer(x2, batch, seq, l0_norm_g, l0_w_in, l0_pool_w, l0_pool_scale, l0_w_out)
    x2 = _odd_layer(x2, batch, seq, l1_norm_g, l1_w_in, l1_b_f, l1_q_norm_g, l1_k_norm_g,
                    l1_ret_norm_g, l1_w_out)
    return x2.reshape(batch, seq, d)
```
